```python
import jax, jax.numpy as jnp
from jax import lax
import numpy as np

D_MODEL = 2048
BATCH = 1
SEQ = 8192
DEPTH = 4

N_MIXERS = 2
GRID_W = 64
N_HEADS = 16
HEAD_DIM = D_MODEL // N_HEADS
WIN_H_MAX = 8
WIN_W = 16
CONV_WIDTH = 31
D_FF = 4 * D_MODEL
N_CONV_LAYERS = (DEPTH + N_MIXERS - 1) // N_MIXERS
N_NA_LAYERS = DEPTH // N_MIXERS
NORM_EPS = 1e-5

kernel_name = "hybrid_conformer_natten_encoder"


def rms_norm(x, g):
    xf = x.astype(jnp.float32)
    y = xf * lax.rsqrt(jnp.mean(xf * xf, axis=-1, keepdims=True) + NORM_EPS)
    return (y * g.astype(jnp.float32)).astype(x.dtype)


def layer_norm(x, g, b):
    xf = x.astype(jnp.float32)
    mu = jnp.mean(xf, axis=-1, keepdims=True)
    xc = xf - mu
    y = xc * lax.rsqrt(jnp.mean(xc * xc, axis=-1, keepdims=True) + NORM_EPS)
    return (y * g.astype(jnp.float32) + b.astype(jnp.float32)).astype(x.dtype)


def conformer_conv(h, w_pw1, b_pw1, w_dw, b_dw, ln_g, ln_b, w_pw2, b_pw2):
    u = h @ w_pw1 + b_pw1
    a, gate = jnp.split(u, 2, axis=-1)
    u = a * jax.nn.sigmoid(gate)
    u = lax.conv_general_dilated(
        u, w_dw[:, None, :], window_strides=(1,),
        padding=[(CONV_WIDTH // 2, CONV_WIDTH // 2)],
        dimension_numbers=("NWC", "WIO", "NWC"),
        feature_group_count=D_MODEL) + b_dw
    u = jax.nn.silu(layer_norm(u, ln_g, ln_b))
    return u @ w_pw2 + b_pw2


def neighbourhood_attention(h, w_qkv, b_qkv, rpb, w_o, b_o):
    bsz, t, _ = h.shape
    rows = t // GRID_W
    kh = min(WIN_H_MAX, rows)
    qkv = (h @ w_qkv + b_qkv).reshape(bsz, rows, GRID_W, 3, N_HEADS, HEAD_DIM)
    q = qkv[:, :, :, 0] * (HEAD_DIM ** -0.5)
    k = qkv[:, :, :, 1]
    v = qkv[:, :, :, 2]
    row_ids = jnp.arange(rows, dtype=jnp.int32)
    row_start = jnp.clip(row_ids - kh // 2, 0, rows - kh)
    col_ids = jnp.arange(GRID_W, dtype=jnp.int32)
    col_start = jnp.clip(col_ids - WIN_W // 2, 0, GRID_W - WIN_W)
    col_idx = col_start[:, None] + jnp.arange(WIN_W, dtype=jnp.int32)[None, :]
    col_bias_idx = col_idx - col_ids[:, None] + (WIN_W - 1)
    rpb_cols = rpb[:, :, col_bias_idx]

    def row_block(r):
        rs = row_start[r]
        q_r = lax.dynamic_index_in_dim(q, r, axis=1, keepdims=False)
        k_band = lax.dynamic_slice_in_dim(k, rs, kh, axis=1)
        v_band = lax.dynamic_slice_in_dim(v, rs, kh, axis=1)
        k_win = k_band[:, :, col_idx]
        v_win = v_band[:, :, col_idx]
        row_bias_idx = rs + jnp.arange(kh, dtype=jnp.int32) - r + (WIN_H_MAX - 1)
        bias = jnp.transpose(rpb_cols[:, row_bias_idx], (0, 2, 1, 3))
        s = jnp.einsum("bqhd,biqjhd->bhqij", q_r, k_win) + bias[None]
        p = jax.nn.softmax(
            s.astype(jnp.float32).reshape(bsz, N_HEADS, GRID_W, kh * WIN_W), axis=-1)
        p = p.reshape(s.shape).astype(v.dtype)
        return jnp.einsum("bhqij,biqjhd->bqhd", p, v_win)

    o = lax.map(row_block, row_ids)
    o = jnp.moveaxis(o, 0, 1).reshape(bsz, t, D_MODEL)
    return o @ w_o + b_o


def sq_relu_mlp(h, w_up, w_down):
    return jnp.square(jax.nn.relu(h @ w_up)) @ w_down


def setup_inputs(seed: int = 0) -> dict:
    key = jax.random.key(seed)
    ks = jax.random.split(key, 24)

    def nrm(k, shape, scale):
        return jax.random.normal(k, shape, jnp.float32) * scale

    d = D_MODEL
    return {
        "x": nrm(ks[0], (BATCH, SEQ, d), 1.0),
        "norm_mix_g": 1.0 + nrm(ks[1], (DEPTH, d), 0.02),
        "norm_ffn_g": 1.0 + nrm(ks[2], (DEPTH, d), 0.02),
        "final_norm_g": 1.0 + nrm(ks[3], (d,), 0.02),
        "conv_w_pw1": nrm(ks[4], (N_CONV_LAYERS, d, 2 * d), d ** -0.5),
        "conv_b_pw1": nrm(ks[5], (N_CONV_LAYERS, 2 * d), 0.02),
        "conv_w_dw": nrm(ks[6], (N_CONV_LAYERS, CONV_WIDTH, d), CONV_WIDTH ** -0.5),
        "conv_b_dw": nrm(ks[7], (N_CONV_LAYERS, d), 0.02),
        "conv_ln_g": 1.0 + nrm(ks[8], (N_CONV_LAYERS, d), 0.02),
        "conv_ln_b": nrm(ks[9], (N_CONV_LAYERS, d), 0.02),
        "conv_w_pw2": nrm(ks[10], (N_CONV_LAYERS, d, d), d ** -0.5),
        "conv_b_pw2": nrm(ks[11], (N_CONV_LAYERS, d), 0.02),
        "na_w_qkv": nrm(ks[12], (N_NA_LAYERS, d, 3 * d), d ** -0.5),
        "na_b_qkv": nrm(ks[13], (N_NA_LAYERS, 3 * d), 0.02),
        "na_rpb": nrm(ks[14], (N_NA_LAYERS, N_HEADS, 2 * WIN_H_MAX - 1, 2 * WIN_W - 1), 0.1),
        "na_w_o": nrm(ks[15], (N_NA_LAYERS, d, d), d ** -0.5),
        "na_b_o": nrm(ks[16], (N_NA_LAYERS, d), 0.02),
        "ffn_w_up": nrm(ks[17], (DEPTH, d, D_FF), d ** -0.5),
        "ffn_w_down": nrm(ks[18], (DEPTH, D_FF, d), D_FF ** -0.5),
    }


def reference(x, norm_mix_g, norm_ffn_g, final_norm_g,
              conv_w_pw1, conv_b_pw1, conv_w_dw, conv_b_dw, conv_ln_g, conv_ln_b,
              conv_w_pw2, conv_b_pw2,
              na_w_qkv, na_b_qkv, na_rpb, na_w_o, na_b_o,
              ffn_w_up, ffn_w_down):
    h = x
    for i in range(DEPTH):
        j = i // N_MIXERS
        hn = rms_norm(h, norm_mix_g[i])
        if i % N_MIXERS == 0:
            h = h + conformer_conv(hn, conv_w_pw1[j], conv_b_pw1[j], conv_w_dw[j],
                                   conv_b_dw[j], conv_ln_g[j], conv_ln_b[j],
                                   conv_w_pw2[j], conv_b_pw2[j])
        else:
            h = h + neighbourhood_attention(hn, na_w_qkv[j], na_b_qkv[j], na_rpb[j],
                                            na_w_o[j], na_b_o[j])
        h = h + sq_relu_mlp(rms_norm(h, norm_ffn_g[i]), ffn_w_up[i], ffn_w_down[i])
    return rms_norm(h, final_norm_g)
```

```python
import functools

import jax
import jax.numpy as jnp
from jax import lax
from jax.experimental import pallas as pl
from jax.experimental.pallas import tpu as pltpu

NORM_EPS = 1e-5
GRID_W = 64
N_HEADS = 16
HEAD_DIM = 128
WIN_H = 8
WIN_W = 16
CONV_WIDTH = 31
CONV_HALO = 16
MASK_VALUE = -1e30
VMEM_LIMIT_BYTES = 58 * 1024 * 1024

BF16 = jnp.bfloat16
F32 = jnp.float32


def _params(n_axes):
    return pltpu.CompilerParams(
        dimension_semantics=("arbitrary",) * n_axes,
        vmem_limit_bytes=VMEM_LIMIT_BYTES)


def _rms_rows(x, g):
    ms = jnp.mean(x * x, axis=-1, keepdims=True)
    return x * lax.rsqrt(ms + NORM_EPS) * g


def _dot(a, b):
    return jnp.dot(a, b, preferred_element_type=F32)


def _norm_rows_into(h_ref, g_ref, hn_s, rm):
    def body(c, carry):
        rr = pl.ds(pl.multiple_of(c * rm, rm), rm)
        hn_s[rr, :] = _rms_rows(h_ref[rr, :], g_ref[...]).astype(BF16)
        return carry
    lax.fori_loop(0, h_ref.shape[0] // rm, body, 0)


def _glu_kernel(h_ref, g_ref, wa_ref, wg_ref, ba_ref, bg_ref, o_ref, hn_s, wa_s, wg_s, *, rm):
    @pl.when(pl.program_id(1) == 0)
    def _():
        _norm_rows_into(h_ref, g_ref, hn_s, rm)

    wa_s[...] = wa_ref[...].astype(BF16)
    wg_s[...] = wg_ref[...].astype(BF16)
    for r in range(h_ref.shape[0] // rm):
        rows = pl.ds(r * rm, rm)
        a = hn_s[rows, :]
        ya = _dot(a, wa_s[...]) + ba_ref[...]
        yg = _dot(a, wg_s[...]) + bg_ref[...]
        o_ref[rows, :] = ya * jax.nn.sigmoid(yg)


def _glu_in(h, g, w, b, *, tm=1024, tn=512, rm=256):
    t, d = h.shape
    nj = d // tn
    b2 = b.reshape(1, 2 * d)
    return pl.pallas_call(
        functools.partial(_glu_kernel, rm=rm),
        grid=(t // tm, nj),
        in_specs=[
            pl.BlockSpec((tm, d), lambda i, j: (i, 0)),
            pl.BlockSpec((1, d), lambda i, j: (0, 0)),
            pl.BlockSpec((d, tn), lambda i, j: (0, j)),
            pl.BlockSpec((d, tn), lambda i, j: (0, j + nj)),
            pl.BlockSpec((1, tn), lambda i, j: (0, j)),
            pl.BlockSpec((1, tn), lambda i, j: (0, j + nj)),
        ],
        out_specs=pl.BlockSpec((tm, tn), lambda i, j: (i, j)),
        out_shape=jax.ShapeDtypeStruct((t, d), F32),
        scratch_shapes=[
            pltpu.VMEM((tm, d), BF16),
            pltpu.VMEM((d, tn), BF16),
            pltpu.VMEM((d, tn), BF16),
        ],
        compiler_params=_params(2),
        name="glu_in",
    )(h, g.reshape(1, d), w, w, b2, b2)


def _qkv_kernel(h_ref, g_ref, w_ref, b_ref, o_ref, hn_s, w_s, *, rm, q_tiles):
    j = pl.program_id(1)

    @pl.when(j == 0)
    def _():
        _norm_rows_into(h_ref, g_ref, hn_s, rm)

    w_s[...] = w_ref[...].astype(BF16)
    scale = jnp.where(j < q_tiles, HEAD_DIM ** -0.5, 1.0).astype(F32)
    for r in range(h_ref.shape[0] // rm):
        rows = pl.ds(r * rm, rm)
        y = (_dot(hn_s[rows, :], w_s[...]) + b_ref[...]) * scale
        for hh in range(o_ref.shape[0]):
            o_ref[hh, rows, :] = y[:, hh * HEAD_DIM:(hh + 1) * HEAD_DIM].astype(BF16)


def _qkv_in(h, g, w, b, *, tm=1024, tn=1024, rm=256):
    t, d = h.shape
    n = w.shape[1]
    hpt = tn // HEAD_DIM
    return pl.pallas_call(
        functools.partial(_qkv_kernel, rm=rm, q_tiles=d // tn),
        grid=(t // tm, n // tn),
        in_specs=[
            pl.BlockSpec((tm, d), lambda i, j: (i, 0)),
            pl.BlockSpec((1, d), lambda i, j: (0, 0)),
            pl.BlockSpec((d, tn), lambda i, j: (0, j)),
            pl.BlockSpec((1, tn), lambda i, j: (0, j)),
        ],
        out_specs=pl.BlockSpec((hpt, tm, HEAD_DIM), lambda i, j: (j, i, 0)),
        out_shape=jax.ShapeDtypeStruct((n // HEAD_DIM, t, HEAD_DIM), BF16),
        scratch_shapes=[
            pltpu.VMEM((tm, d), BF16),
            pltpu.VMEM((d, tn), BF16),
        ],
        compiler_params=_params(2),
        name="qkv_in",
    )(h, g.reshape(1, d), w, b.reshape(1, n))


def _dwconv_kernel(xm_ref, xp_ref, xn_ref, w_ref, bdw_ref, lg_ref, lb_ref, o_ref,
                   xbuf, ybuf):
    i = pl.program_id(0)
    n = pl.num_programs(0)
    tq, d = xm_ref.shape
    halo = CONV_HALO
    n_lane_tiles = d // 128
    nblk = tq // 8
    for l in range(n_lane_tiles):
        lanes = slice(l * 128, (l + 1) * 128)
        xbuf[l, 0:halo, :] = jnp.where(i > 0, xp_ref[:, lanes], 0.0)
        xbuf[l, halo:halo + tq, :] = xm_ref[:, lanes]
        xbuf[l, halo + tq:2 * halo + tq, :] = jnp.where(i < n - 1, xn_ref[:, lanes], 0.0)

    shift = halo - CONV_WIDTH // 2
    row = lax.broadcasted_iota(jnp.int32, (8, 128), 0)
    n_a = (CONV_WIDTH + shift + 7) // 8

    def lane_body(l, carry):
        wv = {k + shift: jnp.broadcast_to(w_ref[l, k:k + 1, :], (8, 128))
              for k in range(CONV_WIDTH)}
        bias = jnp.broadcast_to(bdw_ref[l], (8, 128))

        def rotated_partials(j):
            xs = [xbuf[l, 8 * (j + a):8 * (j + a) + 8, :] for a in range(n_a)]
            out = []
            for b in range(8):
                terms = [wv[8 * a + b] * xs[a] for a in range(n_a) if 8 * a + b in wv]
                pb = functools.reduce(lambda x, y: x + y, terms)
                out.append(pb if b == 0 else pltpu.roll(pb, 8 - b, axis=0))
            return out

        q_cur = rotated_partials(0)
        for j in range(nblk):
            q_next = rotated_partials(j + 1)
            acc = bias + q_cur[0]
            for b in range(1, 8):
                acc = acc + jnp.where(row < 8 - b, q_cur[b], q_next[b])
            ybuf[l, 8 * j:8 * j + 8, :] = acc
            q_cur = q_next
        return carry

    lax.fori_loop(0, n_lane_tiles, lane_body, 0)

    def ln_body(r, carry):
        rr = pl.ds(pl.multiple_of(r * 16, 16), 16)
        y = jnp.concatenate([ybuf[l, rr, :] for l in range(n_lane_tiles)], axis=-1)
        mu = jnp.mean(y, axis=-1, keepdims=True)
        yc = y - mu
        var = jnp.mean(yc * yc, axis=-1, keepdims=True)
        z = yc * lax.rsqrt(var + NORM_EPS) * lg_ref[...] + lb_ref[...]
        o_ref[rr, :] = (z * jax.nn.sigmoid(z)).astype(BF16)
        return carry

    lax.fori_loop(0, tq // 16, ln_body, 0, unroll=4)


def _dwconv_ln_swish(u, w_dw, b_dw, ln_g, ln_b, *, tq=256):
    t, d = u.shape
    hb = tq // CONV_HALO
    last = t // CONV_HALO - 1
    nl = d // 128
    w_tiles = w_dw.reshape(CONV_WIDTH, nl, 128).transpose(1, 0, 2)
    return pl.pallas_call(
        _dwconv_kernel,
        grid=(t // tq,),
        in_specs=[
            pl.BlockSpec((tq, d), lambda i: (i, 0)),
            pl.BlockSpec((CONV_HALO, d), lambda i: (jnp.maximum(i * hb - 1, 0), 0)),
            pl.BlockSpec((CONV_HALO, d), lambda i: (jnp.minimum((i + 1) * hb, last), 0)),
            pl.BlockSpec((nl, CONV_WIDTH, 128), lambda i: (0, 0, 0)),
            pl.BlockSpec((nl, 1, 128), lambda i: (0, 0, 0)),
            pl.BlockSpec((1, d), lambda i: (0, 0)),
            pl.BlockSpec((1, d), lambda i: (0, 0)),
        ],
        out_specs=pl.BlockSpec((tq, d), lambda i: (i, 0)),
        out_shape=jax.ShapeDtypeStruct((t, d), BF16),
        scratch_shapes=[
            pltpu.VMEM((nl, tq + 2 * CONV_HALO, 128), F32),
            pltpu.VMEM((nl, tq, 128), F32),
        ],
        compiler_params=_params(1),
        name="dwconv_ln_swish",
    )(u, u, u, w_tiles, b_dw.reshape(nl, 1, 128), ln_g.reshape(1, d), ln_b.reshape(1, d))


QROWS = WIN_H
BAND = 2 * WIN_H
KBLK_ROWS = 4


def _window_plan(variant):
    if variant == "first":
        return [max(i - WIN_H // 2, 0) for i in range(QROWS)], WIN_H - 1
    if variant == "last":
        return [min(i + WIN_H // 2, BAND - WIN_H) for i in range(QROWS)], -1
    return list(range(QROWS)), WIN_H // 2 - 1


def _attn_heads(q_ref, k_refs, v_refs, p_ref, o_ref, variant):
    lo_rel, a_off = _window_plan(variant)
    w = GRID_W
    pair = 2 * w
    lane = lax.broadcasted_iota(jnp.int32, (w, pair), 1)
    for hh in range(q_ref.shape[0]):
        q = q_ref[hh]
        k = jnp.concatenate([r[hh] for r in k_refs], axis=0)
        v = jnp.concatenate([r[hh] for r in v_refs], axis=0)
        s = lax.dot_general(q, k, (((1,), (1,)), ((), ())), preferred_element_type=F32)
        p_rows, inv_l = [], []
        for i in range(QROWS):
            lo = lo_rel[i]
            tiles = {}
            for m in range(BAND // 2):
                ok_a = lo <= 2 * m < lo + WIN_H
                ok_b = lo <= 2 * m + 1 < lo + WIN_H
                if not (ok_a or ok_b):
                    continue
                a = 2 * m - i + a_off
                assert 0 <= a < p_ref.shape[1]
                blk = s[i * w:(i + 1) * w, m * pair:(m + 1) * pair] + p_ref[hh, a]
                if not (ok_a and ok_b):
                    blk = jnp.where((lane >= w) if ok_b else (lane < w), blk, MASK_VALUE)
                tiles[m] = blk
            mx = functools.reduce(jnp.maximum, tiles.values())
            mx = jnp.max(mx, axis=-1, keepdims=True)
            es = {m: jnp.exp(blk - mx) for m, blk in tiles.items()}
            l = jnp.sum(functools.reduce(lambda x, y: x + y, es.values()), axis=-1, keepdims=True)
            zero = jnp.zeros((w, pair), BF16)
            p_rows.append(jnp.concatenate(
                [es[m].astype(BF16) if m in es else zero for m in range(BAND // 2)], axis=1))
            inv_l.append(1.0 / l)
        p = jnp.concatenate(p_rows, axis=0)
        o = _dot(p, v) * jnp.concatenate(inv_l, axis=0)
        o_ref[hh] = o.astype(BF16)


def _attn_kernel(q_ref, k0, k1, k2, k3, v0, v1, v2, v3, p_ref, o_ref):
    rb = pl.program_id(1)
    nrb = pl.num_programs(1)
    k_refs = (k0, k1, k2, k3)
    v_refs = (v0, v1, v2, v3)

    @pl.when(rb == 0)
    def _():
        _attn_heads(q_ref, k_refs, v_refs, p_ref, o_ref, "first")

    @pl.when(rb == nrb - 1)
    def _():
        _attn_heads(q_ref, k_refs, v_refs, p_ref, o_ref, "last")

    @pl.when(jnp.logical_and(rb > 0, rb < nrb - 1))
    def _():
        _attn_heads(q_ref, k_refs, v_refs, p_ref, o_ref, "mid")


def _bias_table(rpb):
    qc = jnp.arange(GRID_W, dtype=jnp.int32)[:, None]
    kc = jnp.arange(GRID_W, dtype=jnp.int32)[None, :]
    cs = jnp.clip(qc - WIN_W // 2, 0, GRID_W - WIN_W)
    valid = (kc >= cs) & (kc < cs + WIN_W)
    idx = jnp.clip(kc - qc + (WIN_W - 1), 0, 2 * WIN_W - 2)
    colmat = jnp.where(valid[None, None], rpb[:, :, idx], MASK_VALUE)
    return jnp.concatenate([colmat[:, :-1], colmat[:, 1:]], axis=-1)


def _attention(qkv, rpb, *, hg=4):
    nh3, t, dh = qkv.shape
    nh = nh3 // 3
    rows = t // GRID_W
    nrb = rows // QROWS
    tq = QROWS * GRID_W
    tk = KBLK_ROWS * GRID_W
    nkb = BAND // KBLK_ROWS
    ngr = nh // hg
    table = _bias_table(rpb)
    max_kb = (rows - BAND) // KBLK_ROWS

    def kv_spec(which, m):
        def index(g, rb):
            start = jnp.clip(rb * (QROWS // KBLK_ROWS) - (WIN_H // 2) // KBLK_ROWS, 0, max_kb)
            return (which * ngr + g, start + m, 0)
        return pl.BlockSpec((hg, tk, dh), index)

    return pl.pallas_call(
        _attn_kernel,
        grid=(ngr, nrb),
        in_specs=[pl.BlockSpec((hg, tq, dh), lambda g, rb: (g, rb, 0))]
        + [kv_spec(1, m) for m in range(nkb)]
        + [kv_spec(2, m) for m in range(nkb)]
        + [pl.BlockSpec((hg,) + table.shape[1:], lambda g, rb: (g, 0, 0, 0))],
        out_specs=pl.BlockSpec((hg, tq, dh), lambda g, rb: (g, rb, 0)),
        out_shape=jax.ShapeDtypeStruct((nh, t, dh), BF16),
        compiler_params=_params(2),
        name="natten",
    )(qkv, *([qkv] * (2 * nkb)), table)


def _proj_kernel(a_ref, h_ref, w_ref, b_ref, g_ref, o_ref, hn_ref, w_s, *, rm, slab, head_major):
    kk = pl.program_id(1)
    nk = pl.num_programs(1)
    tm = o_ref.shape[0]

    @pl.when(kk == 0)
    def _():
        o_ref[...] = jnp.zeros_like(o_ref)

    w_s[...] = w_ref[...].astype(BF16)
    for r in range(tm // rm):
        rows = pl.ds(r * rm, rm)
        if head_major:
            a = jnp.concatenate([a_ref[hh, rows, :] for hh in range(a_ref.shape[0])], axis=-1)
        else:
            a = a_ref[rows, :]
        o_ref[rows, :] += _dot(a, w_s[...])

    srows = pl.ds(pl.multiple_of(kk * slab, slab), slab)
    o_ref[srows, :] += h_ref[...]

    @pl.when(kk == nk - 1)
    def _():
        def body(c, carry):
            rr = pl.ds(pl.multiple_of(c * rm, rm), rm)
            y = o_ref[rr, :] + b_ref[...]
            o_ref[rr, :] = y
            hn_ref[rr, :] = _rms_rows(y, g_ref[...]).astype(BF16)
            return carry
        lax.fori_loop(0, tm // rm, body, 0)


def _proj_residual(a, h, w, b, g, *, head_major, tm=1024, tk=512, rm=256):
    t, d = h.shape
    nk = w.shape[0] // tk
    slab = tm // nk
    if head_major:
        hpt = tk // HEAD_DIM
        a_spec = pl.BlockSpec((hpt, tm, HEAD_DIM), lambda i, k: (k, i, 0))
    else:
        a_spec = pl.BlockSpec((tm, tk), lambda i, k: (i, k))
    return pl.pallas_call(
        functools.partial(_proj_kernel, rm=rm, slab=slab, head_major=head_major),
        grid=(t // tm, nk),
        in_specs=[
            a_spec,
            pl.BlockSpec((slab, d), lambda i, k: (i * nk + k, 0)),
            pl.BlockSpec((tk, d), lambda i, k: (k, 0)),
            pl.BlockSpec((1, d), lambda i, k: (0, 0)),
            pl.BlockSpec((1, d), lambda i, k: (0, 0)),
        ],
        out_specs=[
            pl.BlockSpec((tm, d), lambda i, k: (i, 0)),
            pl.BlockSpec((tm, d), lambda i, k: (i, 0)),
        ],
        out_shape=[
            jax.ShapeDtypeStruct((t, d), F32),
            jax.ShapeDtypeStruct((t, d), BF16),
        ],
        scratch_shapes=[pltpu.VMEM((tk, d), BF16)],
        compiler_params=_params(2),
        name="proj_residual",
    )(a, h, w, b.reshape(1, d), g.reshape(1, d))


def _ffn_kernel(hn_ref, h_ref, wup_ref, wdn_ref, g_ref, o_ref, wup_s, wdn_s, *,
                rm, slab, final_norm):
    f = pl.program_id(1)
    nf = pl.num_programs(1)
    tm = o_ref.shape[0]

    @pl.when(f == 0)
    def _():
        o_ref[...] = jnp.zeros_like(o_ref)

    wup_s[...] = wup_ref[...].astype(BF16)
    wdn_s[...] = wdn_ref[...].astype(BF16)
    for r in range(tm // rm):
        rows = pl.ds(r * rm, rm)
        u = jnp.maximum(_dot(hn_ref[rows, :], wup_s[...]), 0.0)
        o_ref[rows, :] += _dot((u * u).astype(BF16), wdn_s[...])

    srows = pl.ds(pl.multiple_of(f * slab, slab), slab)
    o_ref[srows, :] += h_ref[...]

    if final_norm:
        @pl.when(f == nf - 1)
        def _():
            def body(c, carry):
                rr = pl.ds(pl.multiple_of(c * rm, rm), rm)
                o_ref[rr, :] = _rms_rows(o_ref[rr, :], g_ref[...])
                return carry
            lax.fori_loop(0, tm // rm, body, 0)


def _ffn(hn, h, w_up, w_down, g, *, final_norm, tm=1024, tf=512, rm=256):
    t, d = h.shape
    nf = w_up.shape[1] // tf
    slab = tm // nf
    return pl.pallas_call(
        functools.partial(_ffn_kernel, rm=rm, slab=slab, final_norm=final_norm),
        grid=(t // tm, nf),
        in_specs=[
            pl.BlockSpec((tm, d), lambda i, f: (i, 0)),
            pl.BlockSpec((slab, d), lambda i, f: (i * nf + f, 0)),
            pl.BlockSpec((d, tf), lambda i, f: (0, f)),
            pl.BlockSpec((tf, d), lambda i, f: (f, 0)),
            pl.BlockSpec((1, d), lambda i, f: (0, 0)),
        ],
        out_specs=pl.BlockSpec((tm, d), lambda i, f: (i, 0)),
        out_shape=jax.ShapeDtypeStruct((t, d), F32),
        scratch_shapes=[
            pltpu.VMEM((d, tf), BF16),
            pltpu.VMEM((tf, d), BF16),
        ],
        compiler_params=_params(2),
        name="ffn",
    )(hn, h, w_up, w_down, g.reshape(1, d))


def kernel(x, norm_mix_g, norm_ffn_g, final_norm_g, conv_w_pw1, conv_b_pw1, conv_w_dw, conv_b_dw, conv_ln_g, conv_ln_b, conv_w_pw2, conv_b_pw2, na_w_qkv, na_b_qkv, na_rpb, na_w_o, na_b_o, ffn_w_up, ffn_w_down):
    bsz, t, d = x.shape
    depth = norm_mix_g.shape[0]
    outs = []
    for bi in range(bsz):
        h = x[bi]
        for i in range(depth):
            j = i // 2
            if i % 2 == 0:
                u = _glu_in(h, norm_mix_g[i], conv_w_pw1[j], conv_b_pw1[j])
                a = _dwconv_ln_swish(u, conv_w_dw[j], conv_b_dw[j], conv_ln_g[j], conv_ln_b[j])
                h, hn = _proj_residual(a, h, conv_w_pw2[j], conv_b_pw2[j], norm_ffn_g[i],
                                       head_major=False)
            else:
                qkv = _qkv_in(h, norm_mix_g[i], na_w_qkv[j], na_b_qkv[j])
                a = _attention(qkv, na_rpb[j])
                h, hn = _proj_residual(a, h, na_w_o[j], na_b_o[j], norm_ffn_g[i],
                                       head_major=True)
            last = i == depth - 1
            h = _ffn(hn, h, ffn_w_up[i], ffn_w_down[i], final_norm_g, final_norm=last)
        outs.append(h)
    return jnp.stack(outs, axis=0)
```
